```python
import jax, jax.numpy as jnp
from jax import lax
import numpy as np

D_MODEL = 1024
BATCH = 32
SEQ = 256
DEPTH = 1
DEC_BATCH = 8
DEC_SEQ = 2048
PAST_LEN = 512

GRID_W = 64
CONV_WIDTH = 512
CONV_K = 3
N_HEADS = 8
HEAD_DIM = 64
ATTN_WIDTH = N_HEADS * HEAD_DIM
WIN_ROWS_MAX = 8
WIN_COLS = 16
D_FF = -(-8 * D_MODEL // (3 * 256)) * 256
N_IN = 3 * CONV_WIDTH + 3 * ATTN_WIDTH
EPS = 1e-6
NEG = -1e30

kernel_name = "hybrid_shortconv_natten_diffusion_step"


def _rmsnorm(x, g):
    xf = x.astype(jnp.float32)
    y = xf * lax.rsqrt(jnp.mean(xf * xf, axis=-1, keepdims=True) + EPS)
    return (y * g.astype(jnp.float32)).astype(x.dtype)


def _modulation(cond, w_mod, b_mod):
    m = jax.nn.silu(cond) @ w_mod + b_mod
    return jnp.split(m, 6, axis=-1)


def _split_heads(t):
    B, L, _ = t.shape
    return t.reshape(B, L, N_HEADS, HEAD_DIM).transpose(0, 2, 1, 3)


def _merge_heads(t):
    B, H, L, Dh = t.shape
    return t.transpose(0, 2, 1, 3).reshape(B, L, H * Dh)


def _short_conv(u, w_conv):
    L = u.shape[1]
    pad = CONV_K // 2
    up = jnp.pad(u, ((0, 0), (pad, pad), (0, 0)))
    out = w_conv[0] * up[:, 0:L]
    for j in range(1, CONV_K):
        out = out + w_conv[j] * up[:, j:j + L]
    return out


def _mixer_inputs(x, shift, scale, g_pre, w_in):
    h = _rmsnorm(x, g_pre) * (1.0 + scale) + shift
    proj = h @ w_in
    cw, aw = CONV_WIDTH, ATTN_WIDTH
    b, cg, xc, q, k, v = jnp.split(
        proj, [cw, 2 * cw, 3 * cw, 3 * cw + aw, 3 * cw + 2 * aw], axis=-1)
    return h, b, cg, xc, _split_heads(q), _split_heads(k), _split_heads(v)


def _merge_branches(h, y_conv, y_attn, w_br_conv, w_br_attn, w_gate, w_o):
    g_conv, g_attn = jnp.split(jax.nn.sigmoid(h @ w_gate), 2, axis=-1)
    branch_conv = y_conv @ w_br_conv
    branch_attn = _merge_heads(y_attn) @ w_br_attn
    return (g_conv * branch_conv + g_attn * branch_attn) @ w_o


def _ffn_sublayer(x, shift, scale, gate, g_pre, g_post, w1, w3, w2):
    h = _rmsnorm(x, g_pre) * (1.0 + scale) + shift
    y = (jax.nn.silu(h @ w1) * (h @ w3)) @ w2
    return x + gate * _rmsnorm(y, g_post)


def _context_attention(q, k, v):
    s = jnp.einsum('bhqd,bhkd->bhqk', q, k).astype(jnp.float32) * (HEAD_DIM ** -0.5)
    p = jax.nn.softmax(s, axis=-1).astype(v.dtype)
    return jnp.einsum('bhqk,bhkd->bhqd', p, v)


def _neighbourhood_attention(q, k, v, k_ctx, v_ctx, rpb):
    B, H, T, Dh = q.shape
    rows = T // GRID_W
    kh = min(WIN_ROWS_MAX, rows)
    kw = WIN_COLS
    qg = q.reshape(B, H, rows, GRID_W, Dh)
    kg = k.reshape(B, H, rows, GRID_W, Dh)
    vg = v.reshape(B, H, rows, GRID_W, Dh)
    r = jnp.arange(rows)
    r0 = jnp.clip(r - kh // 2, 0, rows - kh)
    row_idx = r0[:, None] + jnp.arange(kh)[None, :]
    k_band = kg[:, :, row_idx]
    v_band = vg[:, :, row_idx]
    cq = jnp.arange(GRID_W)
    c0 = jnp.clip(cq - kw // 2, 0, GRID_W - kw)
    ck = jnp.arange(GRID_W)
    col_in = (ck[None, :] >= c0[:, None]) & (ck[None, :] < c0[:, None] + kw)
    row_off = row_idx - r[:, None] + (WIN_ROWS_MAX - 1)
    col_off = jnp.clip(ck[None, :] - cq[:, None], -(WIN_COLS - 1), WIN_COLS - 1) + (WIN_COLS - 1)
    bias = rpb[:, row_off[:, None, :, None], col_off[None, :, None, :]]
    s_nb = jnp.einsum('bhrwd,bhrkxd->bhrwkx', qg, k_band).astype(jnp.float32) * (HEAD_DIM ** -0.5)
    s_nb = s_nb + bias[None].astype(jnp.float32)
    s_nb = jnp.where(col_in[:, None, :], s_nb, NEG)
    s_nb = s_nb.reshape(B, H, rows, GRID_W, kh * GRID_W)
    s_ctx = jnp.einsum('bhrwd,bhpd->bhrwp', qg, k_ctx).astype(jnp.float32) * (HEAD_DIM ** -0.5)
    n_nb = kh * GRID_W
    p = jax.nn.softmax(jnp.concatenate([s_nb, s_ctx], axis=-1), axis=-1).astype(v.dtype)
    p_nb = p[..., :n_nb].reshape(B, H, rows, GRID_W, kh, GRID_W)
    p_ctx = p[..., n_nb:]
    out = (jnp.einsum('bhrwkx,bhrkxd->bhrwd', p_nb, v_band)
           + jnp.einsum('bhrwp,bhpd->bhrwd', p_ctx, v_ctx))
    return out.reshape(B, H, T, Dh)


def setup_inputs(seed: int = 0) -> dict:
    key = jax.random.key(seed)
    ks = jax.random.split(key, 24)
    f32 = jnp.float32

    def nrm(k, shape, scale=1.0):
        return (jax.random.normal(k, shape, f32) * scale).astype(f32)

    L = DEPTH
    return {
        "x_prompt": nrm(ks[0], (BATCH, SEQ, D_MODEL)),
        "x_sample": nrm(ks[1], (DEC_BATCH, DEC_SEQ, D_MODEL)),
        "cache_k": nrm(ks[2], (DEC_BATCH, DEPTH, N_HEADS, PAST_LEN, HEAD_DIM)),
        "cache_v": nrm(ks[3], (DEC_BATCH, DEPTH, N_HEADS, PAST_LEN, HEAD_DIM)),
        "c": nrm(ks[4], (DEC_BATCH, D_MODEL)),
        "c_ctx": nrm(ks[5], (D_MODEL,)),
        "w_mod": nrm(ks[6], (L, D_MODEL, 6 * D_MODEL), D_MODEL ** -0.5),
        "b_mod": nrm(ks[7], (L, 6 * D_MODEL), 0.01),
        "g_pre_mix": 1.0 + nrm(ks[8], (L, D_MODEL), 0.1),
        "g_post_mix": 1.0 + nrm(ks[9], (L, D_MODEL), 0.1),
        "g_pre_ffn": 1.0 + nrm(ks[10], (L, D_MODEL), 0.1),
        "g_post_ffn": 1.0 + nrm(ks[11], (L, D_MODEL), 0.1),
        "w_in": nrm(ks[12], (L, D_MODEL, N_IN), D_MODEL ** -0.5),
        "w_conv": nrm(ks[13], (L, CONV_K, CONV_WIDTH), CONV_K ** -0.5),
        "w_br_conv": nrm(ks[14], (L, CONV_WIDTH, D_MODEL), CONV_WIDTH ** -0.5),
        "w_br_attn": nrm(ks[15], (L, ATTN_WIDTH, D_MODEL), ATTN_WIDTH ** -0.5),
        "rpb": nrm(ks[16], (L, N_HEADS, 2 * WIN_ROWS_MAX - 1, 2 * WIN_COLS - 1), 0.1),
        "w_gate": nrm(ks[17], (L, D_MODEL, 2 * D_MODEL), D_MODEL ** -0.5),
        "w_o": nrm(ks[18], (L, D_MODEL, D_MODEL), D_MODEL ** -0.5),
        "w_ff1": nrm(ks[19], (L, D_MODEL, D_FF), D_MODEL ** -0.5),
        "w_ff3": nrm(ks[20], (L, D_MODEL, D_FF), D_MODEL ** -0.5),
        "w_ff2": nrm(ks[21], (L, D_FF, D_MODEL), D_FF ** -0.5),
    }


def reference(x_prompt, x_sample, cache_k, cache_v, c, c_ctx, w_mod, b_mod,
              g_pre_mix, g_post_mix, g_pre_ffn, g_post_ffn, w_in, w_conv,
              w_br_conv, w_br_attn, rpb, w_gate, w_o, w_ff1, w_ff3, w_ff2):
    y_p = x_prompt
    y_s = x_sample
    new_k = []
    new_v = []
    for l in range(DEPTH):
        sh1, sc1, gt1, sh2, sc2, gt2 = _modulation(c_ctx, w_mod[l], b_mod[l])
        h, b, cg, xc, q, k, v = _mixer_inputs(y_p, sh1, sc1, g_pre_mix[l], w_in[l])
        y_conv = b * _short_conv(cg * xc, w_conv[l])
        y_att = _context_attention(q, k, v)
        mixed = _merge_branches(h, y_conv, y_att, w_br_conv[l], w_br_attn[l], w_gate[l], w_o[l])
        y_p = y_p + gt1 * _rmsnorm(mixed, g_post_mix[l])
        y_p = _ffn_sublayer(y_p, sh2, sc2, gt2, g_pre_ffn[l], g_post_ffn[l],
                            w_ff1[l], w_ff3[l], w_ff2[l])
        new_k.append(k)
        new_v.append(v)

        sh1, sc1, gt1, sh2, sc2, gt2 = _modulation(c[:, None, :], w_mod[l], b_mod[l])
        h, b, cg, xc, q, k, v = _mixer_inputs(y_s, sh1, sc1, g_pre_mix[l], w_in[l])
        y_conv = b * _short_conv(cg * xc, w_conv[l])
        y_att = _neighbourhood_attention(q, k, v, cache_k[:, l], cache_v[:, l], rpb[l])
        mixed = _merge_branches(h, y_conv, y_att, w_br_conv[l], w_br_attn[l], w_gate[l], w_o[l])
        y_s = y_s + gt1 * _rmsnorm(mixed, g_post_mix[l])
        y_s = _ffn_sublayer(y_s, sh2, sc2, gt2, g_pre_ffn[l], g_post_ffn[l],
                            w_ff1[l], w_ff3[l], w_ff2[l])
    new_cache_k = jnp.stack(new_k, axis=1)
    new_cache_v = jnp.stack(new_v, axis=1)
    return (y_p, y_s, new_cache_k, new_cache_v)
```

```python
import functools

import numpy as np
import jax
import jax.numpy as jnp
from jax import lax
from jax.experimental import pallas as pl
from jax.experimental.pallas import tpu as pltpu

D_MODEL = 1024
CONV_WIDTH = 512
CONV_K = 3
N_HEADS = 8
HEAD_DIM = 64
ATTN_WIDTH = N_HEADS * HEAD_DIM
GRID_W = 64
WIN_ROWS = 8
WIN_COLS = 16
EPS = 1e-6
NEG = -1e30

LANES = 128
HEADS_PER_LANE_TILE = LANES // HEAD_DIM
N_PAIRS = N_HEADS // HEADS_PER_LANE_TILE
HALO_ROWS = 8
TOKEN_TILE = 512
FFN_CHUNKS = 2
QROWS_PER_BLOCK = 4
KROWS_PER_BLOCK = QROWS_PER_BLOCK + WIN_ROWS
VMEM_LIMIT_BYTES = 56 * 1024 * 1024

_BF16 = jnp.bfloat16
_F32 = jnp.float32


def _rms(x, g):
    return x * lax.rsqrt(jnp.mean(x * x, axis=-1, keepdims=True) + EPS) * g


def _dot(a, b):
    return jnp.dot(a, b, preferred_element_type=_F32)


def _dot_nt(a, b):
    return lax.dot_general(a, b, (((1,), (1,)), ((), ())), preferred_element_type=_F32)


def _const_spec(shape):
    zeros = (0,) * len(shape)
    return pl.BlockSpec(shape, lambda *_: zeros, pipeline_mode=pl.Buffered(1))


def _params(n_grid_dims):
    return pltpu.CompilerParams(
        dimension_semantics=("arbitrary",) * n_grid_dims,
        vmem_limit_bytes=VMEM_LIMIT_BYTES)


def _mod_kernel(cond_ref, w_ref, b_ref, o_ref):
    cond = cond_ref[...]
    act = cond * jax.nn.sigmoid(cond)
    o_ref[...] = jnp.dot(act, w_ref[...], preferred_element_type=_F32,
                         precision=lax.Precision.HIGHEST) + b_ref[...]


def _modulation(cond, w_mod, b_mod):
    rows = cond.shape[0]
    n_out = w_mod.shape[1]
    blk = D_MODEL
    return pl.pallas_call(
        _mod_kernel,
        grid=(n_out // blk,),
        in_specs=[pl.BlockSpec((rows, D_MODEL), lambda j: (0, 0)),
                  pl.BlockSpec((D_MODEL, blk), lambda j: (0, j)),
                  pl.BlockSpec((1, blk), lambda j: (0, j))],
        out_specs=pl.BlockSpec((rows, blk), lambda j: (0, j)),
        out_shape=jax.ShapeDtypeStruct((rows, n_out), _F32),
        compiler_params=_params(1),
        name="modulation",
    )(cond, w_mod, b_mod.reshape(1, n_out))


def _mixer_in_kernel(x_ref, mod_ref, g_ref, w_in_ref, w_gate_ref, *out_refs, seq_len, with_cache):
    if with_cache:
        u_ref, b_ref, q_ref, k_ref, v_ref, gate_ref, ck_ref, cv_ref = out_refs
    else:
        u_ref, b_ref, q_ref, k_ref, v_ref, gate_ref = out_refs
    x = x_ref[...]
    shift = mod_ref[0, 0:1, :]
    scale = mod_ref[0, 1:2, :]
    h = (_rms(x, g_ref[...]) * (1.0 + scale) + shift).astype(_BF16)

    cw, aw = CONV_WIDTH, ATTN_WIDTH
    b_ref[...] = _dot(h, w_in_ref[:, 0:cw])
    u_ref[...] = _dot(h, w_in_ref[:, cw:2 * cw]) * _dot(h, w_in_ref[:, 2 * cw:3 * cw])
    q_ref[...] = (_dot(h, w_in_ref[:, 3 * cw:3 * cw + aw]) * (HEAD_DIM ** -0.5)).astype(_BF16)
    k = _dot(h, w_in_ref[:, 3 * cw + aw:3 * cw + 2 * aw])
    v = _dot(h, w_in_ref[:, 3 * cw + 2 * aw:3 * cw + 3 * aw])
    k_ref[...] = k.astype(_BF16)
    v_ref[...] = v.astype(_BF16)
    if with_cache:
        for s in range(x.shape[0] // seq_len):
            for hd in range(N_HEADS):
                rows = slice(s * seq_len, (s + 1) * seq_len)
                cols = slice(hd * HEAD_DIM, (hd + 1) * HEAD_DIM)
                ck_ref[s, 0, hd] = k[rows, cols]
                cv_ref[s, 0, hd] = v[rows, cols]
    n_gate = w_gate_ref.shape[1]
    for j in range(n_gate // cw):
        cols = slice(j * cw, (j + 1) * cw)
        gate_ref[:, cols] = jax.nn.sigmoid(_dot(h, w_gate_ref[:, cols])).astype(_BF16)


def _mixer_in(x2d, mod, mod_row, g_pre, w_in, w_gate, *, seq_len, with_cache):
    n_tok = x2d.shape[0]
    tm = TOKEN_TILE
    tok = lambda width: pl.BlockSpec((tm, width), lambda i: (i, 0))
    out_shape = [jax.ShapeDtypeStruct((n_tok, CONV_WIDTH), _F32),
                 jax.ShapeDtypeStruct((n_tok, CONV_WIDTH), _F32),
                 jax.ShapeDtypeStruct((n_tok, ATTN_WIDTH), _BF16),
                 jax.ShapeDtypeStruct((n_tok, ATTN_WIDTH), _BF16),
                 jax.ShapeDtypeStruct((n_tok, ATTN_WIDTH), _BF16),
                 jax.ShapeDtypeStruct((n_tok, 2 * D_MODEL), _BF16)]
    out_specs = [tok(CONV_WIDTH), tok(CONV_WIDTH), tok(ATTN_WIDTH), tok(ATTN_WIDTH),
                 tok(ATTN_WIDTH), tok(2 * D_MODEL)]
    if with_cache:
        n_seq = n_tok // seq_len
        seq_per_tile = tm // seq_len
        cache_shape = (n_seq, 1, N_HEADS, seq_len, HEAD_DIM)
        cache_spec = pl.BlockSpec((seq_per_tile, 1, N_HEADS, seq_len, HEAD_DIM),
                                  lambda i: (i, 0, 0, 0, 0))
        out_shape += [jax.ShapeDtypeStruct(cache_shape, _F32)] * 2
        out_specs += [cache_spec, cache_spec]
    return pl.pallas_call(
        functools.partial(_mixer_in_kernel, seq_len=seq_len, with_cache=with_cache),
        grid=(n_tok // tm,),
        in_specs=[tok(D_MODEL),
                  pl.BlockSpec((1, 6, D_MODEL), lambda i: (mod_row(i * tm), 0, 0)),
                  _const_spec((1, D_MODEL)),
                  _const_spec(w_in.shape),
                  _const_spec(w_gate.shape)],
        out_specs=out_specs,
        out_shape=out_shape,
        compiler_params=_params(1),
        name="mixer_in_ctx" if with_cache else "mixer_in_lat",
    )(x2d, mod, g_pre, w_in, w_gate)


def _head_masks():
    lane = lax.broadcasted_iota(jnp.int32, (1, LANES), 1)
    return [(lane >= hh * HEAD_DIM) & (lane < (hh + 1) * HEAD_DIM)
            for hh in range(HEADS_PER_LANE_TILE)]


def _ctx_attn_kernel(q_ref, k_ref, v_ref, o_ref):
    masks = _head_masks()
    for p in range(N_PAIRS):
        cols = slice(p * LANES, (p + 1) * LANES)
        q = q_ref[:, cols]
        k = k_ref[:, cols]
        v = v_ref[:, cols]
        out = None
        for hh in range(HEADS_PER_LANE_TILE):
            s = _dot_nt(jnp.where(masks[hh], q, jnp.zeros_like(q)), k)
            e = jnp.exp(s - jnp.max(s, axis=-1, keepdims=True))
            o = _dot(e.astype(_BF16), v) / jnp.sum(e, axis=-1, keepdims=True)
            out = o if out is None else jnp.where(masks[hh], o, out)
        o_ref[:, cols] = out.astype(o_ref.dtype)


def _ctx_attention(q, k, v, seq_len):
    n_tok = q.shape[0]
    spec = pl.BlockSpec((seq_len, ATTN_WIDTH), lambda b: (b, 0))
    return pl.pallas_call(
        _ctx_attn_kernel,
        grid=(n_tok // seq_len,),
        in_specs=[spec, spec, spec],
        out_specs=spec,
        out_shape=jax.ShapeDtypeStruct((n_tok, ATTN_WIDTH), _BF16),
        compiler_params=_params(1),
        name="ctx_attention",
    )(q, k, v)


def _nbr_attn_kernel(q_ref, k_ref, v_ref, kc_ref, vc_ref, bias_ref, o_ref, *, rows):
    masks = _head_masks()
    left = lax.broadcasted_iota(jnp.int32, (GRID_W, LANES), 1) < GRID_W
    kc = kc_ref[0]
    vc = vc_ref[0]
    qrows, krows = QROWS_PER_BLOCK, KROWS_PER_BLOCK
    for jb in range(rows // qrows):
        ws = min(max(qrows * jb - WIN_ROWS // 2, 0), rows - krows)
        q = q_ref[jb * qrows * GRID_W:(jb + 1) * qrows * GRID_W, :]
        kb = k_ref[ws * GRID_W:(ws + krows) * GRID_W, :]
        vb = v_ref[ws * GRID_W:(ws + krows) * GRID_W, :]
        out = None
        for hh in range(HEADS_PER_LANE_TILE):
            qh = jnp.where(masks[hh], q, jnp.zeros_like(q))
            s_nb = _dot_nt(qh, kb)
            s_cx = _dot_nt(qh, kc)
            slabs = []
            for i in range(qrows):
                r = qrows * jb + i
                r0 = min(max(r - WIN_ROWS // 2, 0), rows - WIN_ROWS)
                tiles = []
                for j in range(krows // 2):
                    kr = ws + 2 * j
                    d = kr - r
                    lv = r0 <= kr < r0 + WIN_ROWS
                    rv = r0 <= kr + 1 < r0 + WIN_ROWS
                    if not (lv or rv):
                        t = jnp.full((GRID_W, LANES), NEG, _F32)
                    else:
                        t = bias_ref[hh, d + WIN_ROWS]
                        if not lv:
                            t = jnp.where(left, NEG, t)
                        if not rv:
                            t = jnp.where(left, t, NEG)
                    tiles.append(t)
                slabs.append(jnp.concatenate(tiles, axis=1))
            s_nb = s_nb + jnp.concatenate(slabs, axis=0)
            m = jnp.maximum(jnp.max(s_nb, axis=-1, keepdims=True),
                            jnp.max(s_cx, axis=-1, keepdims=True))
            e_nb = jnp.exp(s_nb - m)
            e_cx = jnp.exp(s_cx - m)
            denom = jnp.sum(e_nb, axis=-1, keepdims=True) + jnp.sum(e_cx, axis=-1, keepdims=True)
            o = (_dot(e_nb.astype(_BF16), vb) + _dot(e_cx.astype(_BF16), vc)) / denom
            out = o if out is None else jnp.where(masks[hh], o, out)
        o_ref[jb * qrows * GRID_W:(jb + 1) * qrows * GRID_W, :] = out.astype(o_ref.dtype)


def _nbr_attention(q, k, v, kc, vc, bias_tiles, seq_len):
    n_tok = q.shape[0]
    n_batch = n_tok // seq_len
    past = kc.shape[1]
    tok_spec = pl.BlockSpec((seq_len, LANES), lambda p, b: (b, p))
    ctx_spec = pl.BlockSpec((1, past, LANES), lambda p, b: (b, 0, p))
    bias_spec = pl.BlockSpec((HEADS_PER_LANE_TILE,) + bias_tiles.shape[1:],
                             lambda p, b: (p, 0, 0, 0))
    return pl.pallas_call(
        functools.partial(_nbr_attn_kernel, rows=seq_len // GRID_W),
        grid=(N_PAIRS, n_batch),
        in_specs=[tok_spec, tok_spec, tok_spec, ctx_spec, ctx_spec, bias_spec],
        out_specs=tok_spec,
        out_shape=jax.ShapeDtypeStruct((n_tok, ATTN_WIDTH), _BF16),
        compiler_params=_params(2),
        name="nbr_attention",
    )(q, k, v, kc, vc, bias_tiles)


def _bias_pair_tiles(rpb):
    cq = np.arange(GRID_W)
    c0 = np.clip(cq - WIN_COLS // 2, 0, GRID_W - WIN_COLS)
    ck = np.arange(GRID_W)
    col_in = (ck[None, :] >= c0[:, None]) & (ck[None, :] < c0[:, None] + WIN_COLS)
    col_off = np.clip(ck[None, :] - cq[:, None], -(WIN_COLS - 1), WIN_COLS - 1) + (WIN_COLS - 1)
    t = jnp.take(rpb, jnp.asarray(col_off.reshape(-1)), axis=2)
    t = t.reshape(rpb.shape[0], rpb.shape[1], GRID_W, GRID_W)
    t = jnp.where(jnp.asarray(col_in)[None, None], t, NEG)
    t = jnp.pad(t, ((0, 0), (1, 1), (0, 0), (0, 0)), constant_values=NEG)
    return jnp.concatenate([t[:, :-1], t[:, 1:]], axis=-1)


def _mixer_out_kernel(x_ref, u_ref, up_ref, un_ref, b_ref, ya_ref, gate_ref, mod_ref,
                      wconv_ref, gpost_ref, gpre2_ref, gpost2_ref,
                      wbc_ref, wba_ref, wo_ref, w1_ref, w3_ref, w2_ref, o_ref, *, seq_len):
    tm = x_ref.shape[0]
    x = x_ref[...]
    u = u_ref[...]
    row = lax.broadcasted_iota(jnp.int32, (tm, 1), 0)
    pos = (row + pl.program_id(0) * tm) % seq_len
    u_prev = jnp.where(row == 0, up_ref[HALO_ROWS - 1:HALO_ROWS, :], pltpu.roll(u, 1, 0))
    u_prev = jnp.where(pos == 0, 0.0, u_prev)
    u_next = jnp.where(row == tm - 1, un_ref[0:1, :], pltpu.roll(u, tm - 1, 0))
    u_next = jnp.where(pos == seq_len - 1, 0.0, u_next)
    conv = wconv_ref[0:1, :] * u_prev + wconv_ref[1:2, :] * u + wconv_ref[2:3, :] * u_next
    y_conv = (b_ref[...] * conv).astype(_BF16)

    branch_conv = _dot(y_conv, wbc_ref[...])
    branch_attn = _dot(ya_ref[...], wba_ref[...])
    merged = (gate_ref[:, 0:D_MODEL].astype(_F32) * branch_conv
              + gate_ref[:, D_MODEL:2 * D_MODEL].astype(_F32) * branch_attn)
    mixed = _dot(merged.astype(_BF16), wo_ref[...])
    gate1 = mod_ref[0, 2:3, :]
    shift2 = mod_ref[0, 3:4, :]
    scale2 = mod_ref[0, 4:5, :]
    gate2 = mod_ref[0, 5:6, :]
    x1 = x + gate1 * _rms(mixed, gpost_ref[...])

    h2 = (_rms(x1, gpre2_ref[...]) * (1.0 + scale2) + shift2).astype(_BF16)
    d_ff = w1_ref.shape[1]
    chunk = d_ff // FFN_CHUNKS
    y = None
    for c in range(FFN_CHUNKS):
        cols = slice(c * chunk, (c + 1) * chunk)
        a = _dot(h2, w1_ref[:, cols])
        g = _dot(h2, w3_ref[:, cols])
        act = (a * jax.nn.sigmoid(a) * g).astype(_BF16)
        part = _dot(act, w2_ref[cols, :])
        y = part if y is None else y + part
    o_ref[...] = x1 + gate2 * _rms(y, gpost2_ref[...])


def _mixer_out(x2d, u, b, y_att, gates, mod, mod_row, w_conv, g_post, g_pre2, g_post2,
               w_bc, w_ba, w_o, w1, w3, w2, *, seq_len, name):
    n_tok = x2d.shape[0]
    tm = TOKEN_TILE
    halo_per_tile = tm // HALO_ROWS
    n_halo = n_tok // HALO_ROWS
    tok = lambda width: pl.BlockSpec((tm, width), lambda i: (i, 0))
    prev_spec = pl.BlockSpec((HALO_ROWS, CONV_WIDTH),
                             lambda i: (jnp.maximum(i * halo_per_tile - 1, 0), 0))
    next_spec = pl.BlockSpec((HALO_ROWS, CONV_WIDTH),
                             lambda i: (jnp.minimum((i + 1) * halo_per_tile, n_halo - 1), 0))
    return pl.pallas_call(
        functools.partial(_mixer_out_kernel, seq_len=seq_len),
        grid=(n_tok // tm,),
        in_specs=[tok(D_MODEL), tok(CONV_WIDTH), prev_spec, next_spec, tok(CONV_WIDTH),
                  tok(ATTN_WIDTH), tok(2 * D_MODEL),
                  pl.BlockSpec((1, 6, D_MODEL), lambda i: (mod_row(i * tm), 0, 0)),
                  _const_spec(w_conv.shape), _const_spec((1, D_MODEL)), _const_spec((1, D_MODEL)),
                  _const_spec((1, D_MODEL)),
                  _const_spec(w_bc.shape), _const_spec(w_ba.shape), _const_spec(w_o.shape),
                  _const_spec(w1.shape), _const_spec(w3.shape), _const_spec(w2.shape)],
        out_specs=tok(D_MODEL),
        out_shape=jax.ShapeDtypeStruct((n_tok, D_MODEL), _F32),
        compiler_params=_params(1),
        name=name,
    )(x2d, u, u, u, b, y_att, gates, mod, w_conv, g_post, g_pre2, g_post2,
      w_bc, w_ba, w_o, w1, w3, w2)


def kernel(x_prompt, x_sample, cache_k, cache_v, c, c_ctx, w_mod, b_mod, g_pre_mix, g_post_mix,
           g_pre_ffn, g_post_ffn, w_in, w_conv, w_br_conv, w_br_attn, rpb, w_gate, w_o,
           w_ff1, w_ff3, w_ff2):
    depth = w_mod.shape[0]
    assert depth == 1, "single trunk layer"
    n_ctx, ctx_len, _ = x_prompt.shape
    n_lat, lat_len, _ = x_sample.shape
    past = cache_k.shape[3]
    assert TOKEN_TILE % ctx_len == 0 and lat_len % TOKEN_TILE == 0
    assert (lat_len // GRID_W) % QROWS_PER_BLOCK == 0 and lat_len // GRID_W >= KROWS_PER_BLOCK

    l = 0
    cond_rows = -(-(1 + n_lat) // HALO_ROWS) * HALO_ROWS
    cond = jnp.zeros((cond_rows, D_MODEL), _F32).at[0].set(c_ctx).at[1:1 + n_lat].set(c)
    mod = _modulation(cond, w_mod[l], b_mod[l]).reshape(cond_rows, 6, D_MODEL)

    bf = lambda w: w.astype(_BF16)
    w_in_b, w_gate_b = bf(w_in[l]), bf(w_gate[l])
    w_bc, w_ba, w_o_b = bf(w_br_conv[l]), bf(w_br_attn[l]), bf(w_o[l])
    w1, w3, w2 = bf(w_ff1[l]), bf(w_ff3[l]), bf(w_ff2[l])
    row = lambda g: g[l].reshape(1, D_MODEL)
    g_pre, g_post, g_pre2, g_post2 = row(g_pre_mix), row(g_post_mix), row(g_pre_ffn), row(g_post_ffn)

    ctx_row = lambda tok0: 0
    lat_row = lambda tok0: 1 + tok0 // lat_len

    xp = x_prompt.reshape(n_ctx * ctx_len, D_MODEL)
    u, b, q, k, v, gates, new_k, new_v = _mixer_in(
        xp, mod, ctx_row, g_pre, w_in_b, w_gate_b, seq_len=ctx_len, with_cache=True)
    y_att = _ctx_attention(q, k, v, ctx_len)
    y_p = _mixer_out(xp, u, b, y_att, gates, mod, ctx_row, w_conv[l], g_post, g_pre2, g_post2,
                     w_bc, w_ba, w_o_b, w1, w3, w2, seq_len=ctx_len, name="mixer_out_ctx")

    xs = x_sample.reshape(n_lat * lat_len, D_MODEL)
    u, b, q, k, v, gates = _mixer_in(
        xs, mod, lat_row, g_pre, w_in_b, w_gate_b, seq_len=lat_len, with_cache=False)
    to_lanes = lambda t: bf(t[:, l].transpose(0, 2, 1, 3).reshape(n_lat, past, ATTN_WIDTH))
    y_att = _nbr_attention(q, k, v, to_lanes(cache_k), to_lanes(cache_v),
                           _bias_pair_tiles(rpb[l]), lat_len)
    y_s = _mixer_out(xs, u, b, y_att, gates, mod, lat_row, w_conv[l], g_post, g_pre2, g_post2,
                     w_bc, w_ba, w_o_b, w1, w3, w2, seq_len=lat_len, name="mixer_out_lat")

    return (y_p.reshape(x_prompt.shape), y_s.reshape(x_sample.shape), new_k, new_v)
```
